```python
import math
import jax, jax.numpy as jnp
from jax import lax
import numpy as np

D_MODEL = 1024
BATCH = 32
SEQ = 2048
DEPTH = 1
DEC_BATCH = 128
DEC_SEQ = 4
PAST_LEN = 16384
PAGE_SIZE = 128

H_A = 4
DH_A = 64
DV_A = 2 * DH_A
W_A = H_A * DV_A
H_B = 8
DN_B = 64
DR_B = 32
DV_B = 64
Q_LORA = 384
KV_LORA = 256
W_B = H_B * DV_B
D_MIX = W_A + W_B
D_FF = 4 * D_MODEL
SPLIT_SIZES = (H_A * 2 * DH_A, H_A * 2 * DH_A, H_A * DV_A, Q_LORA, KV_LORA, DR_B)
D_IN = sum(SPLIT_SIZES)
Q_BLOCK = 128
NORM_EPS = 1e-6
SUBLN_EPS = 1e-5
ROPE_THETA = 10000.0
NEG_INF = -1e30
DIFF_SCALE = DH_A ** -0.5
MLA_SCALE = (DN_B + DR_B) ** -0.5

kernel_name = 'hymba_diffattn_mla_sandwich_decode_step'

F32 = jnp.float32


def _rms(x, g, eps=NORM_EPS):
    xf = x.astype(F32)
    y = xf * lax.rsqrt(jnp.mean(xf * xf, axis=-1, keepdims=True) + eps)
    return (y * g.astype(F32)).astype(x.dtype)


def _alibi_slopes(n):
    return jnp.asarray([2.0 ** (-8.0 * (i + 1) / n) for i in range(n)], F32)


def _rope(x, pos):
    half = x.shape[-1] // 2
    inv = ROPE_THETA ** (-jnp.arange(half, dtype=F32) / half)
    ang = pos.astype(F32)[:, None] * inv[None, :]
    shape = (ang.shape[0],) + (1,) * (x.ndim - 3) + (half,)
    c = jnp.cos(ang).reshape(shape)
    s = jnp.sin(ang).reshape(shape)
    xf = x.astype(F32)
    x1, x2 = xf[..., :half], xf[..., half:]
    return jnp.concatenate([x1 * c - x2 * s, x1 * s + x2 * c], axis=-1).astype(x.dtype)


def _init_state(lead, dv):
    return (jnp.full(lead, NEG_INF, F32), jnp.zeros(lead, F32), jnp.zeros(lead + (dv,), F32))


def _merge(state, s, v, eq, mask):
    m, l, acc = state
    if mask is not None:
        s = jnp.where(mask, s, NEG_INF)
    m_new = jnp.maximum(m, jnp.max(s, axis=-1))
    corr = jnp.exp(m - m_new)
    p = jnp.exp(s - m_new[..., None])
    if mask is not None:
        p = jnp.where(mask, p, 0.0)
    l_new = l * corr + jnp.sum(p, axis=-1)
    acc_new = acc * corr[..., None] + jnp.einsum(eq, p.astype(v.dtype), v, preferred_element_type=F32)
    return (m_new, l_new, acc_new)


def _diff_merge(state, q, k, v, pos_q, pos_k, slopes, causal):
    s = jnp.einsum('bqhmd,bkhmd->bhmqk', q, k, preferred_element_type=F32) * DIFF_SCALE
    dist = (pos_q[:, None] - pos_k[None, :]).astype(F32)
    s = s - slopes[:, None, None, None] * dist
    mask = (pos_q[:, None] >= pos_k[None, :]) if causal else None
    return _merge(state, s, v, 'bhmqk,bkhv->bhmqv', mask)


def _mla_merge(state, q_lat, q_r, ckv, kr, pos_q, pos_k, causal):
    s = (jnp.einsum('bqhl,bkl->bhqk', q_lat, ckv, preferred_element_type=F32)
         + jnp.einsum('bqhr,bkr->bhqk', q_r, kr, preferred_element_type=F32)) * MLA_SCALE
    mask = (pos_q[:, None] >= pos_k[None, :]) if causal else None
    return _merge(state, s, ckv, 'bhqk,bkl->bhql', mask)


def _diff_finish(state, lam, lam_init, g_subln):
    _, l, acc = state
    o = acc / l[..., None]
    o = o[:, :, 0] - lam * o[:, :, 1]
    o = _rms(o, g_subln, SUBLN_EPS) * (1.0 - lam_init)
    b, h, t, dv = o.shape
    return jnp.transpose(o, (0, 2, 1, 3)).reshape(b, t, h * dv)


def _mla_finish(state, w_uv):
    _, l, acc = state
    o_lat = acc / l[..., None]
    o = jnp.einsum('bhql,lhv->bqhv', o_lat.astype(w_uv.dtype), w_uv)
    b, t = o.shape[0], o.shape[1]
    return o.reshape(b, t, H_B * DV_B)


def _layer_inputs(x, pos, g_pre, w_in, g_q_a, w_uq, g_kv_a):
    b, t, _ = x.shape
    h = _rms(x, g_pre)
    proj = h @ w_in
    cuts = [int(c) for c in np.cumsum(SPLIT_SIZES)[:-1]]
    q_a, k_a, v_a, c_q, c_kv, k_r = jnp.split(proj, cuts, axis=-1)
    q_a = q_a.reshape(b, t, H_A, 2, DH_A)
    k_a = k_a.reshape(b, t, H_A, 2, DH_A)
    v_a = v_a.reshape(b, t, H_A, DV_A)
    q_b = (_rms(c_q, g_q_a) @ w_uq).reshape(b, t, H_B, DN_B + DR_B)
    q_nope = q_b[..., :DN_B]
    q_rope = _rope(q_b[..., DN_B:], pos)
    ckv = _rms(c_kv, g_kv_a)
    k_rope = _rope(k_r, pos)
    return q_a, k_a, v_a, q_nope, q_rope, ckv, k_rope


def _prompt_attend(q_a, k_a, v_a, q_nope, q_r, ckv, kr, pos, w_uk, w_uv, slopes, lam, lam_init, g_subln):
    b, t = q_a.shape[0], q_a.shape[1]

    def block(start):
        sl = lambda a: lax.dynamic_slice_in_dim(a, start, Q_BLOCK, axis=1)
        pos_q = lax.dynamic_slice_in_dim(pos, start, Q_BLOCK)
        sa = _diff_merge(_init_state((b, H_A, 2, Q_BLOCK), DV_A), sl(q_a), k_a, v_a, pos_q, pos, slopes, True)
        q_lat = jnp.einsum('bqhn,lhn->bqhl', sl(q_nope), w_uk)
        sb = _mla_merge(_init_state((b, H_B, Q_BLOCK), KV_LORA), q_lat, sl(q_r), ckv, kr, pos_q, pos, True)
        return _diff_finish(sa, lam, lam_init, g_subln), _mla_finish(sb, w_uv)

    o_a, o_b = lax.map(block, jnp.arange(0, t, Q_BLOCK, dtype=jnp.int32))
    o_a = jnp.moveaxis(o_a, 0, 1).reshape(b, t, W_A)
    o_b = jnp.moveaxis(o_b, 0, 1).reshape(b, t, W_B)
    return o_a, o_b


def _sample_attend(layer, q_a, k_a, v_a, q_nope, q_r, ckv, kr, pos_new, cache_diff_k, cache_diff_v,
                   cache_mla_ckv, cache_mla_krope, page_table, w_uk, w_uv, slopes, lam, lam_init, g_subln):
    b, tq = q_a.shape[0], q_a.shape[1]
    page = cache_diff_k.shape[2]
    n_pages = page_table.shape[1]
    q_lat = jnp.einsum('bqhn,lhn->bqhl', q_nope, w_uk)

    def step(carry, xs):
        sa, sb = carry
        p_idx, phys = xs
        pos_k = p_idx * page + jnp.arange(page, dtype=jnp.int32)
        sa = _diff_merge(sa, q_a, cache_diff_k[layer, phys], cache_diff_v[layer, phys],
                         pos_new, pos_k, slopes, False)
        sb = _mla_merge(sb, q_lat, q_r, cache_mla_ckv[layer, phys], cache_mla_krope[layer, phys],
                        pos_new, pos_k, False)
        return (sa, sb), None

    init = (_init_state((b, H_A, 2, tq), DV_A), _init_state((b, H_B, tq), KV_LORA))
    (sa, sb), _ = lax.scan(step, init, (jnp.arange(n_pages, dtype=jnp.int32), page_table.T))
    sa = _diff_merge(sa, q_a, k_a, v_a, pos_new, pos_new, slopes, True)
    sb = _mla_merge(sb, q_lat, q_r, ckv, kr, pos_new, pos_new, True)
    return _diff_finish(sa, lam, lam_init, g_subln), _mla_finish(sb, w_uv)


def _layer_tail(x, o_a, o_b, w_out, g_post_mix, g_pre_ffn, w_up, w_down, g_post_ffn):
    o = jnp.concatenate([o_a, o_b], axis=-1).astype(x.dtype)
    x = x + _rms(o @ w_out, g_post_mix)
    u = jnp.square(jax.nn.relu(_rms(x, g_pre_ffn) @ w_up))
    return x + _rms(u @ w_down, g_post_ffn)


def setup_inputs(seed: int = 0) -> dict:
    key = jax.random.key(seed)
    ks = jax.random.split(key, 32)
    n_pages = PAST_LEN // PAGE_SIZE
    n_used = DEC_BATCH * n_pages
    n_pool = n_used + (n_used + 3) // 4
    nrm = lambda k, shape, scale=1.0: jax.random.normal(k, shape, F32) * scale
    gain = lambda k, shape: 1.0 + 0.02 * jax.random.normal(k, shape, F32)
    page_table = jax.random.permutation(ks[6], n_pool)[:n_used].reshape(DEC_BATCH, n_pages).astype(jnp.int32)
    return {
        'x_prompt': nrm(ks[0], (BATCH, SEQ, D_MODEL)),
        'x_sample': nrm(ks[1], (DEC_BATCH, DEC_SEQ, D_MODEL)),
        'cache_diff_k': nrm(ks[2], (DEPTH, n_pool, PAGE_SIZE, H_A, 2, DH_A)),
        'cache_diff_v': nrm(ks[3], (DEPTH, n_pool, PAGE_SIZE, H_A, DV_A)),
        'cache_mla_ckv': nrm(ks[4], (DEPTH, n_pool, PAGE_SIZE, KV_LORA)),
        'cache_mla_krope': nrm(ks[5], (DEPTH, n_pool, PAGE_SIZE, DR_B)),
        'page_table': page_table,
        'g_pre_mix': gain(ks[7], (DEPTH, D_MODEL)),
        'w_in': nrm(ks[8], (DEPTH, D_MODEL, D_IN), D_MODEL ** -0.5),
        'g_q_a': gain(ks[9], (DEPTH, Q_LORA)),
        'w_uq': nrm(ks[10], (DEPTH, Q_LORA, H_B * (DN_B + DR_B)), Q_LORA ** -0.5),
        'g_kv_a': gain(ks[11], (DEPTH, KV_LORA)),
        'w_uk': nrm(ks[12], (DEPTH, KV_LORA, H_B, DN_B), KV_LORA ** -0.5),
        'w_uv': nrm(ks[13], (DEPTH, KV_LORA, H_B, DV_B), KV_LORA ** -0.5),
        'lambda_q1': nrm(ks[14], (DEPTH, DH_A), 0.1),
        'lambda_k1': nrm(ks[15], (DEPTH, DH_A), 0.1),
        'lambda_q2': nrm(ks[16], (DEPTH, DH_A), 0.1),
        'lambda_k2': nrm(ks[17], (DEPTH, DH_A), 0.1),
        'g_subln': gain(ks[18], (DEPTH, DV_A)),
        'w_out': nrm(ks[19], (DEPTH, D_MIX, D_MODEL), D_MIX ** -0.5),
        'g_post_mix': gain(ks[20], (DEPTH, D_MODEL)),
        'g_pre_ffn': gain(ks[21], (DEPTH, D_MODEL)),
        'w_up': nrm(ks[22], (DEPTH, D_MODEL, D_FF), D_MODEL ** -0.5),
        'w_down': nrm(ks[23], (DEPTH, D_FF, D_MODEL), D_FF ** -0.5),
        'g_post_ffn': gain(ks[24], (DEPTH, D_MODEL)),
    }


def reference(x_prompt, x_sample, cache_diff_k, cache_diff_v, cache_mla_ckv, cache_mla_krope, page_table,
              g_pre_mix, w_in, g_q_a, w_uq, g_kv_a, w_uk, w_uv, lambda_q1, lambda_k1, lambda_q2, lambda_k2,
              g_subln, w_out, g_post_mix, g_pre_ffn, w_up, w_down, g_post_ffn):
    seq = x_prompt.shape[1]
    dec_seq = x_sample.shape[1]
    past_len = page_table.shape[1] * cache_diff_k.shape[2]
    pos_p = jnp.arange(seq, dtype=jnp.int32)
    pos_s = past_len + jnp.arange(dec_seq, dtype=jnp.int32)
    slopes = _alibi_slopes(H_A)
    yp, ys = x_prompt, x_sample
    kp, vp, cp, rp = [], [], [], []
    ksm, vsm, csm, rsm = [], [], [], []
    for l in range(DEPTH):
        lam_init = 0.8 - 0.6 * math.exp(-0.3 * l)
        lam = (jnp.exp(jnp.sum(lambda_q1[l].astype(F32) * lambda_k1[l].astype(F32)))
               - jnp.exp(jnp.sum(lambda_q2[l].astype(F32) * lambda_k2[l].astype(F32))) + lam_init)
        qa, ka, va, qn, qr, ckv, kr = _layer_inputs(yp, pos_p, g_pre_mix[l], w_in[l], g_q_a[l], w_uq[l], g_kv_a[l])
        oa, ob = _prompt_attend(qa, ka, va, qn, qr, ckv, kr, pos_p, w_uk[l], w_uv[l], slopes, lam, lam_init, g_subln[l])
        yp = _layer_tail(yp, oa, ob, w_out[l], g_post_mix[l], g_pre_ffn[l], w_up[l], w_down[l], g_post_ffn[l])
        kp.append(ka); vp.append(va); cp.append(ckv); rp.append(kr)
        qa, ka, va, qn, qr, ckv, kr = _layer_inputs(ys, pos_s, g_pre_mix[l], w_in[l], g_q_a[l], w_uq[l], g_kv_a[l])
        oa, ob = _sample_attend(l, qa, ka, va, qn, qr, ckv, kr, pos_s, cache_diff_k, cache_diff_v,
                                cache_mla_ckv, cache_mla_krope, page_table, w_uk[l], w_uv[l],
                                slopes, lam, lam_init, g_subln[l])
        ys = _layer_tail(ys, oa, ob, w_out[l], g_post_mix[l], g_pre_ffn[l], w_up[l], w_down[l], g_post_ffn[l])
        ksm.append(ka); vsm.append(va); csm.append(ckv); rsm.append(kr)
    return (yp, ys, jnp.stack(kp), jnp.stack(vp), jnp.stack(cp), jnp.stack(rp),
            jnp.stack(ksm), jnp.stack(vsm), jnp.stack(csm), jnp.stack(rsm))
```

```python
import functools
import math

import jax
import jax.numpy as jnp
import numpy as np
from jax import lax
from jax.experimental import pallas as pl
from jax.experimental.pallas import tpu as pltpu

F32 = jnp.float32
BF16 = jnp.bfloat16

D_MODEL = 1024
H_A = 4
DH_A = 64
DV_A = 2 * DH_A
W_A = H_A * DV_A
H_B = 8
DN_B = 64
DR_B = 32
DV_B = 64
Q_LORA = 384
KV_LORA = 256
W_B = H_B * DV_B
D_FF = 4 * D_MODEL
NORM_EPS = 1e-6
SUBLN_EPS = 1e-5
ROPE_THETA = 10000.0
NEG_INF = -1e30
DIFF_SCALE = DH_A ** -0.5
MLA_SCALE = (DN_B + DR_B) ** -0.5
LANES = 128
SUBLANES = 8
HEAD_SLOT = 128
ROPE_OFF = DN_B

C_QA = 0
C_KA = C_QA + W_A
C_VA = C_KA + W_A
C_CQ = C_VA + W_A
C_CKV = C_CQ + Q_LORA
C_KR = C_CKV + KV_LORA
C_KRROT = C_KR + HEAD_SLOT
C_END = C_KRROT + HEAD_SLOT

VMEM_LIMIT = 56 * 1024 * 1024


def _dot(a, b):
    return jnp.dot(a, b, preferred_element_type=F32)


def _dot_nt(a, b):
    return lax.dot_general(a, b, (((1,), (1,)), ((), ())), preferred_element_type=F32)


def _rms(x, g, eps):
    return x * lax.rsqrt(jnp.mean(x * x, axis=-1, keepdims=True) + eps) * g


def _lam(lq1, lk1, lq2, lk2, lam_init):
    a = jnp.sum(lq1 * lk1, axis=-1, keepdims=True)
    b = jnp.sum(lq2 * lk2, axis=-1, keepdims=True)
    return jnp.exp(a) - jnp.exp(b) + lam_init


def _slope_of_head(hidx):
    out = jnp.zeros(hidx.shape, F32)
    for h in range(H_A):
        out = jnp.where(hidx == h, 2.0 ** (-8.0 * (h + 1) / H_A), out)
    return out


def _proj_kernel(x_ref, tab_ref, gpre_ref, w1_ref, gq_ref, wuq1_ref, wuq2_ref, gkv_ref, wkn_ref, wuv_ref,
                 qa_ref, k32_ref, k16_ref, v32_ref, v16_ref, qm_ref, ckv32_ref, kr_ref, km_ref, vm_ref):
    h = _rms(x_ref[...], gpre_ref[...], NORM_EPS).astype(BF16)
    qa_ref[...] = _dot(h, w1_ref[:, C_QA:C_KA]).astype(BF16)
    k = _dot(h, w1_ref[:, C_KA:C_VA])
    k32_ref[...] = k
    k16_ref[...] = k.astype(BF16)
    v = _dot(h, w1_ref[:, C_VA:C_CQ])
    v32_ref[...] = v
    v16_ref[...] = v.astype(BF16)

    cos_q = tab_ref[:, 0 * LANES:1 * LANES]
    sin_q = tab_ref[:, 1 * LANES:2 * LANES]
    cos_k = tab_ref[:, 2 * LANES:3 * LANES]
    sin_k = tab_ref[:, 3 * LANES:4 * LANES]

    cq = _rms(_dot(h, w1_ref[:, C_CQ:C_CKV]), gq_ref[...], NORM_EPS).astype(BF16)
    qb = _dot(cq, wuq1_ref[...])
    qb_rot = _dot(cq, wuq2_ref[...])
    for hh in range(H_B):
        sl = slice(hh * HEAD_SLOT, (hh + 1) * HEAD_SLOT)
        qm_ref[:, sl] = (qb[:, sl] * cos_q + qb_rot[:, sl] * sin_q).astype(BF16)

    ckv = _rms(_dot(h, w1_ref[:, C_CKV:C_KR]), gkv_ref[...], NORM_EPS)
    ckv32_ref[...] = ckv
    ckv16 = ckv.astype(BF16)
    kr = _dot(h, w1_ref[:, C_KR:C_KRROT]) * cos_k + _dot(h, w1_ref[:, C_KRROT:C_END]) * sin_k
    kr_ref[...] = kr
    kn = _dot(ckv16, wkn_ref[...])
    for hh in range(H_B):
        sl = slice(hh * HEAD_SLOT, (hh + 1) * HEAD_SLOT)
        km_ref[:, sl] = (kn[:, sl] + kr).astype(BF16)
    vm_ref[...] = _dot(ckv16, wuv_ref[...]).astype(BF16)


def _proj(x2d, tab, gpre, w1, gq, wuq1, wuq2, gkv, wkn, wuv, tm):
    n = x2d.shape[0]
    tab_blocks = tab.shape[0] // tm
    row = lambda i: (i, 0)
    const = lambda i: (0, 0)
    full = lambda a: pl.BlockSpec(a.shape, const)
    out_dims = [(W_A, BF16), (W_A, F32), (W_A, BF16), (W_A, F32), (W_A, BF16), (H_B * HEAD_SLOT, BF16),
                (KV_LORA, F32), (HEAD_SLOT, F32), (H_B * HEAD_SLOT, BF16), (W_B, BF16)]
    return pl.pallas_call(
        _proj_kernel,
        grid=(n // tm,),
        in_specs=[pl.BlockSpec((tm, D_MODEL), row),
                  pl.BlockSpec((tm, 4 * LANES), lambda i: (i % tab_blocks, 0)),
                  full(gpre), full(w1), full(gq), full(wuq1), full(wuq2), full(gkv), full(wkn), full(wuv)],
        out_specs=[pl.BlockSpec((tm, d), row) for d, _ in out_dims],
        out_shape=[jax.ShapeDtypeStruct((n, d), t) for d, t in out_dims],
        compiler_params=pltpu.CompilerParams(dimension_semantics=("arbitrary",), vmem_limit_bytes=VMEM_LIMIT),
        name="proj",
    )(x2d, tab, gpre, w1, gq, wuq1, wuq2, gkv, wkn, wuv)


def _flash_unit(q2, load_k, load_v, dv, i, tq, bias_of_block):
    rows = 2 * tq

    def step(kb, carry, masked):
        m, l, acc = carry
        s = _dot_nt(q2, load_k(kb))
        if bias_of_block is not None:
            s = s + bias_of_block(kb)
        if masked:
            r = lax.broadcasted_iota(jnp.int32, (rows, tq), 0)
            c = lax.broadcasted_iota(jnp.int32, (rows, tq), 1)
            r = jnp.where(r >= tq, r - tq, r)
            s = jnp.where(c <= r, s, NEG_INF)
        m_new = jnp.maximum(m, jnp.max(s, axis=-1, keepdims=True))
        alpha = jnp.exp(m - m_new)
        p = jnp.exp(s - m_new)
        l = alpha * l + jnp.sum(p, axis=-1, keepdims=True)
        acc = alpha * acc + _dot(p.astype(BF16), load_v(kb))
        return m_new, l, acc

    init = (jnp.full((rows, 1), NEG_INF, F32), jnp.zeros((rows, 1), F32), jnp.zeros((rows, dv), F32))
    carry = lax.fori_loop(0, i, lambda kb, c: step(kb, c, False), init)
    _, l, acc = step(i, carry, True)
    return acc * (1.0 / l)


def _prompt_attn_kernel(qa_ref, ka_ref, va_ref, qm_ref, km_ref, vm_ref, lq1_ref, lk1_ref, lq2_ref, lk2_ref,
                        gsub_ref, o_ref, *, tq, lam_init):
    i = pl.program_id(1)
    lam = _lam(lq1_ref[...], lk1_ref[...], lq2_ref[...], lk2_ref[...], lam_init)
    lane = lax.broadcasted_iota(jnp.int32, (tq, LANES), 1)
    kcol = lax.broadcasted_iota(jnp.int32, (1, tq), 1)

    def rows_of(kb):
        return pl.ds(pl.multiple_of(kb * tq, tq), tq)

    for h in range(H_A):
        sl = slice(h * DV_A, (h + 1) * DV_A)
        q = qa_ref[:, sl]
        zero = jnp.zeros_like(q)
        q2 = jnp.concatenate([jnp.where(lane < DH_A, q, zero), jnp.where(lane >= DH_A, q, zero)], axis=0)
        slope = 2.0 ** (-8.0 * (h + 1) / H_A)
        bias = lambda kb, slope=slope: slope * ((kb - i) * tq + kcol).astype(F32)
        o = _flash_unit(q2, lambda kb, sl=sl: ka_ref[rows_of(kb), sl], lambda kb, sl=sl: va_ref[rows_of(kb), sl],
                        DV_A, i, tq, bias)
        o = o[:tq] - lam * o[tq:]
        o = _rms(o, gsub_ref[...], SUBLN_EPS) * (1.0 - lam_init)
        o_ref[:, sl] = o.astype(o_ref.dtype)

    for j in range(H_B // 2):
        sk = slice(j * 2 * HEAD_SLOT, (j + 1) * 2 * HEAD_SLOT)
        sv = slice(j * 2 * DV_B, (j + 1) * 2 * DV_B)
        q = qm_ref[:, sk]
        lane2 = lax.broadcasted_iota(jnp.int32, (tq, 2 * HEAD_SLOT), 1)
        zero = jnp.zeros_like(q)
        q2 = jnp.concatenate([jnp.where(lane2 < HEAD_SLOT, q, zero), jnp.where(lane2 >= HEAD_SLOT, q, zero)], axis=0)
        o = _flash_unit(q2, lambda kb, sk=sk: km_ref[rows_of(kb), sk], lambda kb, sv=sv: vm_ref[rows_of(kb), sv],
                        2 * DV_B, i, tq, None)
        o = jnp.where(lane < DV_B, o[:tq], o[tq:])
        o_ref[:, W_A + j * 2 * DV_B:W_A + (j + 1) * 2 * DV_B] = o.astype(o_ref.dtype)


def _prompt_attn(qa, ka, va, qm, km, vm, lq1, lk1, lq2, lk2, gsub, nb, t, tq, lam_init):
    qblk = lambda b, i: (b * (t // tq) + i, 0)
    kvblk = lambda b, i: (b, 0)
    const = lambda b, i: (0, 0)
    small = lambda a: pl.BlockSpec(a.shape, const)
    return pl.pallas_call(
        functools.partial(_prompt_attn_kernel, tq=tq, lam_init=lam_init),
        grid=(nb, t // tq),
        in_specs=[pl.BlockSpec((tq, W_A), qblk), pl.BlockSpec((t, W_A), kvblk), pl.BlockSpec((t, W_A), kvblk),
                  pl.BlockSpec((tq, H_B * HEAD_SLOT), qblk), pl.BlockSpec((t, H_B * HEAD_SLOT), kvblk),
                  pl.BlockSpec((t, W_B), kvblk),
                  small(lq1), small(lk1), small(lq2), small(lk2), small(gsub)],
        out_specs=pl.BlockSpec((tq, W_A + W_B), qblk),
        out_shape=jax.ShapeDtypeStruct((nb * t, W_A + W_B), BF16),
        compiler_params=pltpu.CompilerParams(dimension_semantics=("arbitrary", "arbitrary"),
                                             vmem_limit_bytes=VMEM_LIMIT),
        name="prompt_attn",
    )(qa, ka, va, qm, km, vm, lq1, lk1, lq2, lk2, gsub)


def _sample_q_lat_kernel(qm_ref, wlat_ref, o_ref):
    for h in range(H_B):
        o_ref[:, h * KV_LORA:(h + 1) * KV_LORA] = _dot(
            qm_ref[:, h * HEAD_SLOT:(h + 1) * HEAD_SLOT], wlat_ref[h]).astype(o_ref.dtype)


def _sample_q_lat(qm, wlat):
    n = qm.shape[0]
    return pl.pallas_call(
        _sample_q_lat_kernel,
        out_shape=jax.ShapeDtypeStruct((n, H_B * KV_LORA), BF16),
        compiler_params=pltpu.CompilerParams(vmem_limit_bytes=VMEM_LIMIT),
        name="sample_q_lat",
    )(qm, wlat)


def _mla_finish_kernel(x_ref, w_ref, o_ref):
    o_ref[...] = _dot(x_ref[...], w_ref[...]).astype(o_ref.dtype)


def _mla_finish(olat, wbd):
    n = olat.shape[0]
    return pl.pallas_call(
        _mla_finish_kernel,
        out_shape=jax.ShapeDtypeStruct((n, W_B), BF16),
        compiler_params=pltpu.CompilerParams(vmem_limit_bytes=VMEM_LIMIT),
        name="mla_finish",
    )(olat, wbd)


def _sample_attn_kernel(pt_ref, qd_ref, ql_ref, qr_ref, lq1_ref, lk1_ref, lq2_ref, lk2_ref, gsub_ref,
                        knew_ref, vnew_ref, cnew_ref, rnew_ref, *rest, g, n_pages, page, dec, lam_init):
    del pt_ref
    kt_refs, v_refs, c_refs, r_refs = rest[0:g], rest[g:2 * g], rest[2 * g:3 * g], rest[3 * g:4 * g]
    oa_ref, ol_ref = rest[4 * g], rest[4 * g + 1]
    md_ref, ld_ref, ad_ref, mm_ref, lm_ref, am_ref = rest[4 * g + 2:]
    s_idx = pl.program_id(1)
    n_steps = n_pages // g
    past = n_pages * page
    rows_d = H_A * 2 * SUBLANES
    rows_m = dec * H_B

    @pl.when(s_idx == 0)
    def _():
        md_ref[...] = jnp.full(md_ref.shape, NEG_INF, F32)
        ld_ref[...] = jnp.zeros(ld_ref.shape, F32)
        ad_ref[...] = jnp.zeros(ad_ref.shape, F32)
        mm_ref[...] = jnp.full(mm_ref.shape, NEG_INF, F32)
        lm_ref[...] = jnp.zeros(lm_ref.shape, F32)
        am_ref[...] = jnp.zeros(am_ref.shape, F32)

    rd = lax.broadcasted_iota(jnp.int32, (rows_d, 1), 0)
    slope_d = _slope_of_head(rd // (2 * SUBLANES))
    tq_d = rd % SUBLANES
    tq_m = lax.broadcasted_iota(jnp.int32, (rows_m, 1), 0) // H_B
    kcol = lax.broadcasted_iota(jnp.int32, (1, page), 1)

    def scores(kt_ref, c_ref, r_ref, kpos0, masked):
        sd = jnp.concatenate(
            [_dot(qd_ref[h], kt_ref[h * DV_A:(h + 1) * DV_A, :].astype(BF16)) for h in range(H_A)], axis=0)
        dist = (past + tq_d - (kpos0 + kcol)).astype(F32)
        sd = sd - slope_d * dist
        sm = _dot_nt(ql_ref[...], c_ref[...].astype(BF16)) + _dot(qr_ref[...], r_ref[...].astype(BF16))
        if masked:
            sd = jnp.where(kcol <= tq_d, sd, NEG_INF)
            sm = jnp.where(kcol <= tq_m, sm, NEG_INF)
        return sd, sm

    def update(blocks, kpos0s, masked):
        sds, sms = zip(*[scores(kt, c, r, kp, masked) for (kt, _, c, r), kp in zip(blocks, kpos0s)])
        sd = jnp.concatenate(sds, axis=1)
        sm = jnp.concatenate(sms, axis=1)

        m_new = jnp.maximum(md_ref[...], jnp.max(sd, axis=-1, keepdims=True))
        alpha = jnp.exp(md_ref[...] - m_new)
        p = jnp.exp(sd - m_new)
        ld_ref[...] = alpha * ld_ref[...] + jnp.sum(p, axis=-1, keepdims=True)
        md_ref[...] = m_new
        pb = p.astype(BF16)
        acc = alpha * ad_ref[...]
        for n, (_, v_ref, _, _) in enumerate(blocks):
            pn = pb[:, n * page:(n + 1) * page]
            acc = acc + jnp.concatenate(
                [_dot(pn[h * 2 * SUBLANES:(h + 1) * 2 * SUBLANES], v_ref[pl.ds(h, page, stride=H_A), :].astype(BF16))
                 for h in range(H_A)], axis=0)
        ad_ref[...] = acc

        m_new = jnp.maximum(mm_ref[...], jnp.max(sm, axis=-1, keepdims=True))
        alpha = jnp.exp(mm_ref[...] - m_new)
        p = jnp.exp(sm - m_new)
        lm_ref[...] = alpha * lm_ref[...] + jnp.sum(p, axis=-1, keepdims=True)
        mm_ref[...] = m_new
        pb = p.astype(BF16)
        acc = alpha * am_ref[...]
        for n, (_, _, c_ref, _) in enumerate(blocks):
            acc = acc + _dot(pb[:, n * page:(n + 1) * page], c_ref[...].astype(BF16))
        am_ref[...] = acc

    update(list(zip(kt_refs, v_refs, c_refs, r_refs)), [(s_idx * g + n) * page for n in range(g)], False)

    @pl.when(s_idx == n_steps - 1)
    def _():
        update([(knew_ref, vnew_ref, cnew_ref, rnew_ref)], [past], True)
        lam = _lam(lq1_ref[...], lk1_ref[...], lq2_ref[...], lk2_ref[...], lam_init)
        od = ad_ref[...] * (1.0 / ld_ref[...])
        for h in range(H_A):
            base = h * 2 * SUBLANES
            o = od[base:base + SUBLANES] - lam * od[base + SUBLANES:base + 2 * SUBLANES]
            oa_ref[h * SUBLANES:(h + 1) * SUBLANES, :] = _rms(o, gsub_ref[...], SUBLN_EPS) * (1.0 - lam_init)
        ol_ref[...] = am_ref[...] * (1.0 / lm_ref[...])


def _sample_attn(page_table, qd, ql, qr, lq1, lk1, lq2, lk2, gsub, knew, vnew, cnew, rnew,
                 cache_kt, cache_v, cache_c, cache_rt, g, lam_init):
    nb, n_pages = page_table.shape
    page = cache_c.shape[1]
    dec = ql.shape[1] // H_B
    rows_d, rows_m = H_A * 2 * SUBLANES, dec * H_B
    per_b = lambda a: pl.BlockSpec((None,) + a.shape[1:], lambda b, s, pt: (b,) + (0,) * (a.ndim - 1))
    const = lambda a: pl.BlockSpec(a.shape, lambda b, s, pt: (0,) * a.ndim)

    def paged(a, n):
        return pl.BlockSpec((None,) + a.shape[1:], lambda b, s, pt, n=n: (pt[b, s * g + n], 0, 0))

    in_specs = [per_b(qd), per_b(ql), per_b(qr), const(lq1), const(lk1), const(lq2), const(lk2), const(gsub),
                per_b(knew), per_b(vnew), per_b(cnew), per_b(rnew)]
    for a in (cache_kt, cache_v, cache_c, cache_rt):
        in_specs += [paged(a, n) for n in range(g)]
    grid_spec = pltpu.PrefetchScalarGridSpec(
        num_scalar_prefetch=1,
        grid=(nb, n_pages // g),
        in_specs=in_specs,
        out_specs=[pl.BlockSpec((None, H_A * SUBLANES, DV_A), lambda b, s, pt: (b, 0, 0)),
                   pl.BlockSpec((None, rows_m, KV_LORA), lambda b, s, pt: (b, 0, 0))],
        scratch_shapes=[pltpu.VMEM((rows_d, 1), F32), pltpu.VMEM((rows_d, 1), F32), pltpu.VMEM((rows_d, DV_A), F32),
                        pltpu.VMEM((rows_m, 1), F32), pltpu.VMEM((rows_m, 1), F32),
                        pltpu.VMEM((rows_m, KV_LORA), F32)],
    )
    return pl.pallas_call(
        functools.partial(_sample_attn_kernel, g=g, n_pages=n_pages, page=page, dec=dec, lam_init=lam_init),
        grid_spec=grid_spec,
        out_shape=[jax.ShapeDtypeStruct((nb, H_A * SUBLANES, DV_A), F32),
                   jax.ShapeDtypeStruct((nb, rows_m, KV_LORA), F32)],
        compiler_params=pltpu.CompilerParams(dimension_semantics=("arbitrary", "arbitrary"),
                                             vmem_limit_bytes=VMEM_LIMIT),
        name="sample_attn",
    )(page_table, qd, ql, qr, lq1, lk1, lq2, lk2, gsub, knew, vnew, cnew, rnew,
      *([cache_kt] * g), *([cache_v] * g), *([cache_c] * g), *([cache_rt] * g))


def _tail_kernel(x_ref, o_ref, wout_ref, g1_ref, g2_ref, wup_ref, wdn_ref, g3_ref, y_ref, *, ff_chunk):
    x1 = x_ref[...] + _rms(_dot(o_ref[...], wout_ref[...]), g1_ref[...], NORM_EPS)
    xn = _rms(x1, g2_ref[...], NORM_EPS).astype(BF16)
    acc = jnp.zeros(x1.shape, F32)
    for c in range(D_FF // ff_chunk):
        u = jnp.maximum(_dot(xn, wup_ref[:, c * ff_chunk:(c + 1) * ff_chunk]), 0.0)
        acc = acc + _dot((u * u).astype(BF16), wdn_ref[c * ff_chunk:(c + 1) * ff_chunk, :])
    y_ref[...] = x1 + _rms(acc, g3_ref[...], NORM_EPS)


def _tail(x2d, o2d, wout, g1, g2, wup, wdn, g3, tm):
    n = x2d.shape[0]
    row = lambda i: (i, 0)
    const = lambda a: pl.BlockSpec(a.shape, lambda i: (0, 0), pipeline_mode=pl.Buffered(1))
    return pl.pallas_call(
        functools.partial(_tail_kernel, ff_chunk=1024),
        grid=(n // tm,),
        in_specs=[pl.BlockSpec((tm, D_MODEL), row), pl.BlockSpec((tm, W_A + W_B), row),
                  const(wout), const(g1), const(g2), const(wup), const(wdn), const(g3)],
        out_specs=pl.BlockSpec((tm, D_MODEL), row),
        out_shape=jax.ShapeDtypeStruct((n, D_MODEL), F32),
        compiler_params=pltpu.CompilerParams(dimension_semantics=("arbitrary",), vmem_limit_bytes=VMEM_LIMIT),
        name="tail",
    )(x2d, o2d, wout, g1, g2, wup, wdn, g3)


def _rope_tables(pos):
    half = DR_B // 2
    inv = ROPE_THETA ** (-jnp.arange(half, dtype=F32) / half)
    ang = pos.astype(F32)[:, None] * inv[None, :]
    c, s = jnp.cos(ang), jnp.sin(ang)
    n = pos.shape[0]
    z_lo, z_hi = jnp.zeros((n, ROPE_OFF), F32), jnp.zeros((n, HEAD_SLOT - ROPE_OFF - DR_B), F32)
    cos_k = jnp.concatenate([z_lo, c, c, z_hi], axis=1)
    sin_k = jnp.concatenate([z_lo, s, s, z_hi], axis=1)
    cos_q = MLA_SCALE * jnp.concatenate([jnp.ones((n, ROPE_OFF), F32), c, c, z_hi], axis=1)
    sin_q = MLA_SCALE * sin_k
    return jnp.concatenate([cos_q, sin_q, cos_k, sin_k], axis=1)


def _rot_cols(w):
    half = DR_B // 2
    return jnp.concatenate([-w[..., half:], w[..., :half]], axis=-1)


def _slot(w, off):
    pad = [(0, 0)] * (w.ndim - 1) + [(off, HEAD_SLOT - off - w.shape[-1])]
    return jnp.pad(w, pad)


def _prep_weights(w_in, w_uq, w_uk, w_uv):
    cuts = np.cumsum([W_A, W_A, W_A, Q_LORA, KV_LORA])
    wq, wk, wv, wcq, wckv, wkr = jnp.split(w_in, cuts, axis=-1)
    w1 = jnp.concatenate([wq * DIFF_SCALE, wk, wv, wcq, wckv, _slot(wkr, ROPE_OFF), _slot(_rot_cols(wkr), ROPE_OFF)],
                         axis=-1).astype(BF16)
    wuq = w_uq.reshape(Q_LORA, H_B, DN_B + DR_B)
    wuq1 = _slot(wuq, 0).reshape(Q_LORA, H_B * HEAD_SLOT).astype(BF16)
    wuq2 = _slot(_rot_cols(wuq[..., DN_B:]), ROPE_OFF).reshape(Q_LORA, H_B * HEAD_SLOT).astype(BF16)
    wkn = _slot(w_uk, 0).reshape(KV_LORA, H_B * HEAD_SLOT).astype(BF16)
    wuv = w_uv.reshape(KV_LORA, W_B).astype(BF16)
    wlat = jnp.pad(jnp.transpose(w_uk, (1, 2, 0)), ((0, 0), (0, HEAD_SLOT - DN_B), (0, 0))).astype(BF16)
    eye = jnp.eye(H_B, dtype=w_uv.dtype)
    wbd = jnp.einsum('lhv,hg->hlgv', w_uv, eye).reshape(H_B * KV_LORA, W_B).astype(BF16)
    return w1, wuq1, wuq2, wkn, wuv, wlat, wbd


def kernel(x_prompt, x_sample, cache_diff_k, cache_diff_v, cache_mla_ckv, cache_mla_krope, page_table, g_pre_mix,
           w_in, g_q_a, w_uq, g_kv_a, w_uk, w_uv, lambda_q1, lambda_k1, lambda_q2, lambda_k2, g_subln, w_out,
           g_post_mix, g_pre_ffn, w_up, w_down, g_post_ffn):
    depth = w_in.shape[0]
    assert depth == 1, "single-layer step only"
    nb, t, _ = x_prompt.shape
    ns, dec, _ = x_sample.shape
    n_pool, page = cache_diff_k.shape[1], cache_diff_k.shape[2]
    n_pages = page_table.shape[1]
    past = n_pages * page
    lam_init = 0.8 - 0.6 * math.exp(-0.3 * 0)
    tm = min(512, t)
    tq = min(256, t)
    pages_per_step = 8 if n_pages % 8 == 0 else 1

    w1, wuq1, wuq2, wkn, wuv, wlat, wbd = _prep_weights(w_in[0], w_uq[0], w_uk[0], w_uv[0])
    wout, wup, wdn = w_out[0].astype(BF16), w_up[0].astype(BF16), w_down[0].astype(BF16)
    vec = lambda a: a.reshape(1, -1)
    lams = (vec(lambda_q1[0]), vec(lambda_k1[0]), vec(lambda_q2[0]), vec(lambda_k2[0]))
    gsub = vec(g_subln[0])
    proj_w = (vec(g_pre_mix[0]), w1, vec(g_q_a[0]), wuq1, wuq2, vec(g_kv_a[0]), wkn, wuv)
    tail_w = (wout, vec(g_post_mix[0]), vec(g_pre_ffn[0]), wup, wdn, vec(g_post_ffn[0]))

    xp = x_prompt.reshape(nb * t, D_MODEL)
    tab_p = _rope_tables(jnp.arange(t, dtype=jnp.int32))
    qa, k32, k16, v32, v16, qm, ckv32, kr, km, vm = _proj(xp, tab_p, *proj_w, tm=tm)
    o_p = _prompt_attn(qa, k16, v16, qm, km, vm, *lams, gsub, nb=nb, t=t, tq=tq, lam_init=lam_init)
    y_p = _tail(xp, o_p, *tail_w, tm=tm).reshape(nb, t, D_MODEL)
    kr32 = kr[:, ROPE_OFF:ROPE_OFF + DR_B]

    xs = x_sample.reshape(ns * dec, D_MODEL)
    tab_s = _rope_tables(jnp.tile(past + jnp.arange(dec, dtype=jnp.int32), ns))
    qa_s, k32_s, _, v32_s, _, qm_s, ckv32_s, kr_s, _, _ = _proj(xs, tab_s, *proj_w, tm=ns * dec)
    kr32_s = kr_s[:, ROPE_OFF:ROPE_OFF + DR_B]
    qt = jnp.transpose(qa_s.reshape(ns, dec, H_A, 2, DH_A), (0, 2, 3, 1, 4))
    qt = jnp.pad(qt, ((0, 0), (0, 0), (0, 0), (0, SUBLANES - dec), (0, 0)))
    zq = jnp.zeros_like(qt[:, :, 0])
    qd = jnp.stack([jnp.concatenate([qt[:, :, 0], zq], axis=-1), jnp.concatenate([zq, qt[:, :, 1]], axis=-1)], axis=2)
    qd = qd.reshape(ns, H_A, 2 * SUBLANES, DV_A)
    ql = _sample_q_lat(qm_s, wlat).reshape(ns, dec * H_B, KV_LORA)
    qr = qm_s.reshape(ns, dec, H_B, HEAD_SLOT)[..., ROPE_OFF:ROPE_OFF + DR_B].reshape(ns, dec * H_B, DR_B)
    pad_t = page - dec
    knew = jnp.pad(jnp.transpose(k32_s.reshape(ns, dec, W_A), (0, 2, 1)), ((0, 0), (0, 0), (0, pad_t)))
    vnew = jnp.pad(v32_s.reshape(ns, dec * H_A, DV_A), ((0, 0), (0, pad_t * H_A), (0, 0)))
    cnew = jnp.pad(ckv32_s.reshape(ns, dec, KV_LORA), ((0, 0), (0, pad_t), (0, 0)))
    rnew = jnp.pad(jnp.transpose(kr32_s.reshape(ns, dec, DR_B), (0, 2, 1)), ((0, 0), (0, 0), (0, pad_t)))
    cache_kt = jnp.transpose(cache_diff_k[0], (0, 2, 3, 4, 1)).reshape(n_pool, W_A, page)
    cache_v = cache_diff_v[0].reshape(n_pool, page * H_A, DV_A)
    cache_c = cache_mla_ckv[0]
    cache_rt = jnp.transpose(cache_mla_krope[0], (0, 2, 1))
    oa_s, ol_s = _sample_attn(page_table, qd, ql, qr, *lams, gsub, knew, vnew, cnew, rnew,
                              cache_kt, cache_v, cache_c, cache_rt, g=pages_per_step, lam_init=lam_init)
    oa_s = jnp.transpose(oa_s.reshape(ns, H_A, SUBLANES, DV_A)[:, :, :dec], (0, 2, 1, 3)).reshape(ns * dec, W_A)
    ob_s = _mla_finish(ol_s.reshape(ns * dec, H_B * KV_LORA).astype(BF16), wbd)
    o_s = jnp.concatenate([oa_s.astype(BF16), ob_s], axis=-1)
    y_s = _tail(xs, o_s, *tail_w, tm=ns * dec).reshape(ns, dec, D_MODEL)

    return (y_p, y_s,
            k32.reshape(1, nb, t, H_A, 2, DH_A), v32.reshape(1, nb, t, H_A, DV_A),
            ckv32.reshape(1, nb, t, KV_LORA), kr32.reshape(1, nb, t, DR_B),
            k32_s.reshape(1, ns, dec, H_A, 2, DH_A), v32_s.reshape(1, ns, dec, H_A, DV_A),
            ckv32_s.reshape(1, ns, dec, KV_LORA), kr32_s.reshape(1, ns, dec, DR_B))
```
